```python
import math
import jax
import jax.numpy as jnp
from jax import lax
import numpy as np

D_MODEL = 1024
BATCH = 4
SEQ = 4096
DEPTH = 1
DEC_BATCH = 128
DEC_SEQ = 8
PAST_LEN = 8192
PAGE_SIZE = 128

DN_HEADS = 4
DN_DK = 128
DN_DV = 128
DN_CONV = 4
DN_CHUNK = 64
DF_HEADS = 4
DF_DK = 64
DF_KD = 2 * DF_DK
DF_DV = 128
Q_BLOCK = 128
N_GROUPS = 4
EXPERTS_PER_GROUP = 4
N_EXPERTS = N_GROUPS * EXPERTS_PER_GROUP
TOP_K_IN_GROUP = 2
D_EXPERT = 512
NORM_EPS = 1e-6
N_BRANCHES = 2

DN_QK_W = DN_HEADS * DN_DK
DN_V_W = DN_HEADS * DN_DV
DN_CONV_W = 2 * DN_QK_W + DN_V_W
DF_QK_W = DF_HEADS * DF_KD
DF_V_W = DF_HEADS * DF_DV
IN_SPLITS = (DN_QK_W, DN_QK_W, DN_V_W, DN_V_W, DN_HEADS, DN_HEADS, DF_QK_W, DF_QK_W, DF_V_W, N_BRANCHES * D_MODEL)
IN_COLS = sum(IN_SPLITS)
IN_OFFSETS = tuple(int(o) for o in np.cumsum(IN_SPLITS)[:-1])

kernel_name = 'hybrid_deltanet_diffattn_hmoe_step'


def rms_norm(x, g):
    xf = x.astype(jnp.float32)
    y = xf * lax.rsqrt(jnp.mean(xf * xf, axis=-1, keepdims=True) + NORM_EPS)
    return (y * g.astype(jnp.float32)).astype(x.dtype)


def l2_normalize(x):
    xf = x.astype(jnp.float32)
    return xf * lax.rsqrt(jnp.sum(xf * xf, axis=-1, keepdims=True) + NORM_EPS)


def alibi_slopes():
    return 2.0 ** (-8.0 * jnp.arange(1, DF_HEADS + 1, dtype=jnp.float32) / DF_HEADS)


def causal_short_conv(x, buf, w):
    L = x.shape[1]
    xp = jnp.concatenate([buf.astype(x.dtype), x], axis=1)
    y = sum(xp[:, i:i + L] * w[i].astype(x.dtype) for i in range(DN_CONV))
    return y, xp[:, L:]


def gated_delta_chunked(q, k, v, g, beta, s0):
    bsz, L = q.shape[:2]
    C = min(DN_CHUNK, L)
    n = -(-L // C)
    pad = n * C - L

    def to_chunks(a):
        a = jnp.pad(a, [(0, 0), (0, pad)] + [(0, 0)] * (a.ndim - 2))
        a = a.reshape(bsz, n, C, *a.shape[2:])
        return jnp.moveaxis(a, 3, 1)

    q, k, v, g, beta = (to_chunks(a) for a in (q, k, v, g, beta))
    G = jnp.cumsum(g, axis=-1)
    incl = jnp.tril(jnp.ones((C, C), dtype=bool))
    strict = jnp.tril(jnp.ones((C, C), dtype=bool), -1)
    decay = jnp.exp(jnp.where(incl, G[..., :, None] - G[..., None, :], -jnp.inf))
    kb = k * beta[..., None]
    m = jnp.where(strict, jnp.einsum('bhncd,bhnsd->bhncs', kb, k) * decay, 0.0)
    a_mat = m + jnp.eye(C, dtype=jnp.float32)
    rhs = jnp.concatenate([v * beta[..., None], kb * jnp.exp(G)[..., None]], axis=-1)
    sol = lax.linalg.triangular_solve(a_mat, rhs, left_side=True, lower=True, unit_diagonal=True)
    u0, w = sol[..., :DN_DV], sol[..., DN_DV:]
    qk = jnp.einsum('bhncd,bhnsd->bhncs', q, k) * decay
    q_dec = q * jnp.exp(G)[..., None]
    k_tail = k * jnp.exp(G[..., -1:] - G)[..., None]
    g_tot = jnp.exp(G[..., -1])

    def step(S, inp):
        u0_c, w_c, qk_c, qd_c, kt_c, gt_c = inp
        u = u0_c - jnp.einsum('bhcd,bhde->bhce', w_c, S)
        o = jnp.einsum('bhcd,bhde->bhce', qd_c, S) + jnp.einsum('bhcs,bhse->bhce', qk_c, u)
        S = S * gt_c[..., None, None] + jnp.einsum('bhcd,bhce->bhde', kt_c, u)
        return S, o

    xs = tuple(jnp.moveaxis(a, 2, 0) for a in (u0, w, qk, q_dec, k_tail, g_tot))
    s_fin, o = lax.scan(step, s0, xs)
    o = jnp.moveaxis(jnp.moveaxis(o, 0, 2), 1, 3).reshape(bsz, n * C, DN_HEADS, DN_DV)[:, :L]
    return o, s_fin


def gated_delta_branch(q_in, k_in, v_in, z, a, b, conv_buf, s0, conv_w, a_log, dt_bias, norm_g):
    bsz, L = q_in.shape[:2]
    qkv = jnp.concatenate([q_in, k_in, v_in], axis=-1)
    conv, new_buf = causal_short_conv(qkv, conv_buf, conv_w)
    conv = jax.nn.silu(conv)
    q, k, v = jnp.split(conv, [DN_QK_W, 2 * DN_QK_W], axis=-1)
    q = l2_normalize(q.reshape(bsz, L, DN_HEADS, DN_DK)) * (DN_DK ** -0.5)
    k = l2_normalize(k.reshape(bsz, L, DN_HEADS, DN_DK))
    v = v.reshape(bsz, L, DN_HEADS, DN_DV).astype(jnp.float32)
    g = -jnp.exp(a_log.astype(jnp.float32)) * jax.nn.softplus(a.astype(jnp.float32) + dt_bias.astype(jnp.float32))
    beta = jax.nn.sigmoid(b.astype(jnp.float32))
    o, s_new = gated_delta_chunked(q, k, v, g, beta, s0.astype(jnp.float32))
    o = rms_norm(o, norm_g) * jax.nn.silu(z.reshape(bsz, L, DN_HEADS, DN_DV).astype(jnp.float32))
    return o.reshape(bsz, L, DN_V_W).astype(q_in.dtype), new_buf, s_new


def diff_softmax_pair(q, k, v, q_pos, k_pos, lam):
    s = jnp.einsum('bqhmd,bkhmd->bhmqk', q, k).astype(jnp.float32) * (DF_DK ** -0.5)
    dist = (q_pos[:, None] - k_pos[None, :]).astype(jnp.float32)
    bias = jnp.where(dist >= 0, -alibi_slopes()[:, None, None] * dist, -jnp.inf)
    p = jax.nn.softmax(s + bias[None, :, None], axis=-1)
    w = p[:, :, 0] - lam * p[:, :, 1]
    return jnp.einsum('bhqk,bkhe->bqhe', w.astype(v.dtype), v)


def prompt_attend(q, k, v, lam):
    bsz, L = q.shape[:2]
    nb = L // Q_BLOCK
    qb = jnp.moveaxis(q.reshape(bsz, nb, Q_BLOCK, DF_HEADS, 2, DF_DK), 1, 0)
    k_pos = jnp.arange(L)
    starts = jnp.arange(nb) * Q_BLOCK

    def block(args):
        qi, st = args
        return diff_softmax_pair(qi, k, v, st + jnp.arange(Q_BLOCK), k_pos, lam)

    out = lax.map(block, (qb, starts))
    return jnp.moveaxis(out, 0, 1).reshape(bsz, L, DF_HEADS, DF_DV)


def make_paged_attend(cache_k, cache_v, layer, page_table):
    def attend(q, k, v, lam):
        def one_seq(args):
            qi, ki, vi, pages = args
            kp = cache_k[layer, pages].reshape(-1, DF_HEADS, 2, DF_DK).astype(ki.dtype)
            vp = cache_v[layer, pages].reshape(-1, DF_HEADS, DF_DV).astype(vi.dtype)
            past = kp.shape[0]
            t = qi.shape[0]
            kk = jnp.concatenate([kp, ki], axis=0)
            vv = jnp.concatenate([vp, vi], axis=0)
            o = diff_softmax_pair(qi[None], kk[None], vv[None], past + jnp.arange(t), jnp.arange(past + t), lam)
            return o[0]
        return lax.map(one_seq, (q, k, v, page_table))
    return attend


def hier_moe(h, w_group, b_group, w_router, b_router, w_gate_up, w_down):
    shp = h.shape
    t = h.reshape(-1, shp[-1])
    g_logits = (t @ w_group + b_group).astype(jnp.float32)
    g_prob = jax.nn.softmax(g_logits, axis=-1)
    _, g_idx = lax.top_k(g_logits, 1)
    g_w = jnp.take_along_axis(g_prob, g_idx, axis=-1)
    e_logits = (t @ w_router + b_router).astype(jnp.float32).reshape(-1, N_GROUPS, EXPERTS_PER_GROUP)
    e_in = jnp.take_along_axis(e_logits, g_idx[:, :, None], axis=1)[:, 0]
    e_top, e_idx = lax.top_k(e_in, TOP_K_IN_GROUP)
    e_w = jax.nn.softmax(e_top, axis=-1) * g_w
    expert_id = g_idx * EXPERTS_PER_GROUP + e_idx
    gates = jnp.sum(jax.nn.one_hot(expert_id, N_EXPERTS, dtype=jnp.float32) * e_w[..., None], axis=1)
    out = jnp.zeros_like(t)
    for e in range(N_EXPERTS):
        gate_part, up_part = jnp.split(t @ w_gate_up[e], 2, axis=-1)
        out = out + gates[:, e:e + 1].astype(t.dtype) * ((jax.nn.silu(gate_part) * up_part) @ w_down[e])
    return out.reshape(shp)


def trunk_layer(x, conv_buf, s0, attend, lw, lam_init):
    bsz, L, _ = x.shape
    n = rms_norm(x, lw['norm_mix_g'])
    proj = n @ lw['w_in']
    q_dn, k_dn, v_dn, z_dn, a_dn, b_dn, q_df, k_df, v_df, gate_logits = jnp.split(proj, IN_OFFSETS, axis=-1)
    y_a, new_buf, s_new = gated_delta_branch(q_dn, k_dn, v_dn, z_dn, a_dn, b_dn, conv_buf, s0,
                                             lw['conv_w'], lw['dn_a_log'], lw['dn_dt_bias'], lw['dn_norm_g'])
    f32 = jnp.float32
    lam = (jnp.exp(jnp.sum(lw['lq1'].astype(f32) * lw['lk1'].astype(f32)))
           - jnp.exp(jnp.sum(lw['lq2'].astype(f32) * lw['lk2'].astype(f32))) + lam_init)
    k_rows = k_df.reshape(bsz, L, DF_HEADS, DF_KD)
    v_rows = v_df.reshape(bsz, L, DF_HEADS, DF_DV)
    o_b = attend(q_df.reshape(bsz, L, DF_HEADS, 2, DF_DK), k_rows.reshape(bsz, L, DF_HEADS, 2, DF_DK), v_rows, lam)
    o_b = rms_norm(o_b, lw['df_norm_g']) * (1.0 - lam_init)
    gate = jax.nn.sigmoid(gate_logits.astype(f32)).astype(x.dtype)
    g_a, g_b = jnp.split(gate, N_BRANCHES, axis=-1)
    mixed = g_a * (y_a @ lw['w_branch_a']) + g_b * (o_b.reshape(bsz, L, DF_V_W) @ lw['w_branch_b'])
    h = x + mixed @ lw['w_out']
    h = h + hier_moe(rms_norm(h, lw['norm_ffn_g']), lw['w_group'], lw['b_group'], lw['w_router'],
                     lw['b_router'], lw['w_gate_up'], lw['w_down'])
    return h, new_buf, s_new, k_rows, v_rows


def setup_inputs(seed: int = 0) -> dict:
    key = jax.random.key(seed)
    ks = jax.random.split(key, 32)
    f32 = jnp.float32
    n_pages = PAST_LEN // PAGE_SIZE
    n_used = DEC_BATCH * n_pages
    n_pool = n_used + max(1, n_used // 4)

    def nrm(k, shape, scale):
        return jax.random.normal(k, shape, f32) * scale

    def gain(k, shape):
        return 1.0 + 0.05 * jax.random.normal(k, shape, f32)

    page_table = jax.random.permutation(ks[4], n_pool)[:n_used].reshape(DEC_BATCH, n_pages).astype(jnp.int32)
    dt = jnp.exp(jax.random.uniform(ks[9], (DEPTH, DN_HEADS), f32, math.log(1e-3), math.log(1e-1)))
    return {
        'x_prompt': nrm(ks[0], (BATCH, SEQ, D_MODEL), 1.0),
        'x_sample': nrm(ks[1], (DEC_BATCH, DEC_SEQ, D_MODEL), 1.0),
        'cache_k': nrm(ks[2], (DEPTH, n_pool, PAGE_SIZE, DF_HEADS, DF_KD), 1.0),
        'cache_v': nrm(ks[3], (DEPTH, n_pool, PAGE_SIZE, DF_HEADS, DF_DV), 1.0),
        'page_table': page_table,
        'state_delta': nrm(ks[5], (DEPTH, DEC_BATCH, DN_HEADS, DN_DK, DN_DV), 0.1),
        'state_conv': nrm(ks[6], (DEPTH, DEC_BATCH, DN_CONV - 1, DN_CONV_W), 1.0),
        'norm_mix_g': gain(ks[7], (DEPTH, D_MODEL)),
        'w_in': nrm(ks[8], (DEPTH, D_MODEL, IN_COLS), D_MODEL ** -0.5),
        'conv_w': nrm(ks[10], (DEPTH, DN_CONV, DN_CONV_W), DN_CONV ** -0.5),
        'dn_a_log': jnp.log(jax.random.uniform(ks[11], (DEPTH, DN_HEADS), f32, 1.0, 16.0)),
        'dn_dt_bias': dt + jnp.log(-jnp.expm1(-dt)),
        'dn_norm_g': gain(ks[12], (DEPTH, DN_DV)),
        'df_lambda_q1': nrm(ks[13], (DEPTH, DF_DK), 0.1),
        'df_lambda_k1': nrm(ks[14], (DEPTH, DF_DK), 0.1),
        'df_lambda_q2': nrm(ks[15], (DEPTH, DF_DK), 0.1),
        'df_lambda_k2': nrm(ks[16], (DEPTH, DF_DK), 0.1),
        'df_norm_g': gain(ks[17], (DEPTH, DF_DV)),
        'w_branch_a': nrm(ks[18], (DEPTH, DN_V_W, D_MODEL), DN_V_W ** -0.5),
        'w_branch_b': nrm(ks[19], (DEPTH, DF_V_W, D_MODEL), DF_V_W ** -0.5),
        'w_out': nrm(ks[20], (DEPTH, D_MODEL, D_MODEL), D_MODEL ** -0.5),
        'norm_ffn_g': gain(ks[21], (DEPTH, D_MODEL)),
        'w_group': nrm(ks[22], (DEPTH, D_MODEL, N_GROUPS), D_MODEL ** -0.5),
        'b_group': nrm(ks[23], (DEPTH, N_GROUPS), 0.01),
        'w_router': nrm(ks[24], (DEPTH, D_MODEL, N_EXPERTS), D_MODEL ** -0.5),
        'b_router': nrm(ks[25], (DEPTH, N_EXPERTS), 0.01),
        'w_gate_up': nrm(ks[26], (DEPTH, N_EXPERTS, D_MODEL, 2 * D_EXPERT), D_MODEL ** -0.5),
        'w_down': nrm(ks[27], (DEPTH, N_EXPERTS, D_EXPERT, D_MODEL), D_EXPERT ** -0.5),
        'final_norm_g': gain(ks[28], (D_MODEL,)),
    }


def reference(x_prompt, x_sample, cache_k, cache_v, page_table, state_delta, state_conv,
              norm_mix_g, w_in, conv_w, dn_a_log, dn_dt_bias, dn_norm_g,
              df_lambda_q1, df_lambda_k1, df_lambda_q2, df_lambda_k2, df_norm_g,
              w_branch_a, w_branch_b, w_out, norm_ffn_g, w_group, b_group, w_router, b_router,
              w_gate_up, w_down, final_norm_g):
    sd_p, sc_p, k_p, v_p, sd_s, sc_s, k_s, v_s = ([] for _ in range(8))
    hp, hs = x_prompt, x_sample
    b_p = x_prompt.shape[0]
    for l in range(DEPTH):
        lw = {
            'norm_mix_g': norm_mix_g[l], 'w_in': w_in[l], 'conv_w': conv_w[l],
            'dn_a_log': dn_a_log[l], 'dn_dt_bias': dn_dt_bias[l], 'dn_norm_g': dn_norm_g[l],
            'lq1': df_lambda_q1[l], 'lk1': df_lambda_k1[l], 'lq2': df_lambda_q2[l], 'lk2': df_lambda_k2[l],
            'df_norm_g': df_norm_g[l], 'w_branch_a': w_branch_a[l], 'w_branch_b': w_branch_b[l],
            'w_out': w_out[l], 'norm_ffn_g': norm_ffn_g[l], 'w_group': w_group[l], 'b_group': b_group[l],
            'w_router': w_router[l], 'b_router': b_router[l], 'w_gate_up': w_gate_up[l], 'w_down': w_down[l],
        }
        lam_init = 0.8 - 0.6 * math.exp(-0.3 * l)
        buf0 = jnp.zeros((b_p, DN_CONV - 1, DN_CONV_W), x_prompt.dtype)
        s0 = jnp.zeros((b_p, DN_HEADS, DN_DK, DN_DV), jnp.float32)
        hp, c1, s1, k1, v1 = trunk_layer(hp, buf0, s0, prompt_attend, lw, lam_init)
        paged_attend = make_paged_attend(cache_k, cache_v, l, page_table)
        hs, c2, s2, k2, v2 = trunk_layer(hs, state_conv[l], state_delta[l], paged_attend, lw, lam_init)
        sd_p.append(s1)
        sc_p.append(c1)
        k_p.append(k1)
        v_p.append(v1)
        sd_s.append(s2)
        sc_s.append(c2)
        k_s.append(k2)
        v_s.append(v2)
    y_prompt = rms_norm(hp, final_norm_g)
    y_sample = rms_norm(hs, final_norm_g)
    return (y_prompt, y_sample, jnp.stack(sd_p), jnp.stack(sc_p), jnp.stack(k_p), jnp.stack(v_p),
            jnp.stack(sd_s), jnp.stack(sc_s), jnp.stack(k_s), jnp.stack(v_s))
```

```python
import functools
import math

import jax
import jax.numpy as jnp
from jax import lax
from jax.experimental import pallas as pl
from jax.experimental.pallas import tpu as pltpu

F32 = jnp.float32
BF16 = jnp.bfloat16

D_MODEL = 1024
PAGE_SIZE = 128
DN_HEADS = 4
DN_DK = 128
DN_DV = 128
DN_CONV = 4
DN_CHUNK = 64
DF_HEADS = 4
DF_DK = 64
DF_DV = 128
N_GROUPS = 4
EXPERTS_PER_GROUP = 4
N_EXPERTS = 16
D_EXPERT = 512
NORM_EPS = 1e-6
LAM_INIT = 0.8 - 0.6 * math.exp(-0.3 * 0)

HEAD_W = 128
DN_W = DN_HEADS * HEAD_W
DF_W = DF_HEADS * HEAD_W
QKVZ_W = 4 * DN_W
CONV_W = 3 * DN_W
GATE_W = 2 * D_MODEL
AB_W = 128
STACK = DN_HEADS * DN_CHUNK
VMEM_LIMIT = 52 * 1024 * 1024


def _cparams(sem):
    return pltpu.CompilerParams(dimension_semantics=sem, vmem_limit_bytes=VMEM_LIMIT)


def _split_bf16(a):
    hi = a.astype(BF16)
    lo = (a - hi.astype(F32)).astype(BF16)
    return hi, lo


def _mm(a, b, dims=(((1,), (0,)), ((), ())), passes=1):
    if passes == 1:
        return lax.dot_general(a.astype(BF16), b.astype(BF16), dims, preferred_element_type=F32)
    a_hi, a_lo = _split_bf16(a)
    b_hi, b_lo = _split_bf16(b)
    out = lax.dot_general(a_hi, b_hi, dims, preferred_element_type=F32)
    out = out + lax.dot_general(a_hi, b_lo, dims, preferred_element_type=F32)
    out = out + lax.dot_general(a_lo, b_hi, dims, preferred_element_type=F32)
    return out


NT = (((1,), (1,)), ((), ()))
TN = (((0,), (0,)), ((), ()))


def _rms(x, g):
    return x * lax.rsqrt(jnp.mean(x * x, axis=-1, keepdims=True) + NORM_EPS) * g


def _silu(x):
    return x * jax.nn.sigmoid(x)


def _in_proj_kernel(x_ref, g_ref, w_ref, gates_ref, qkvz_ref, qdf_ref, kdf_ref, vdf_ref, ab_ref):
    xn = _rms(x_ref[...], g_ref[...]).astype(BF16)

    def proj(off, out_ref):
        width = out_ref.shape[1]
        for c in range(0, width, 512):
            w = min(512, width - c)
            out_ref[:, c:c + w] = jnp.dot(xn, w_ref[:, off + c:off + c + w], preferred_element_type=F32)
        return off + width

    off = 0
    for ref in (gates_ref, qkvz_ref, qdf_ref, kdf_ref, vdf_ref, ab_ref):
        off = proj(off, ref)


def _in_proj(x, g, w_cat, tm):
    t = x.shape[0]
    widths = (GATE_W, QKVZ_W, DF_W, DF_W, DF_W, AB_W)
    n_cols = sum(widths)
    return pl.pallas_call(
        _in_proj_kernel,
        grid=(t // tm,),
        in_specs=[
            pl.BlockSpec((tm, D_MODEL), lambda i: (i, 0)),
            pl.BlockSpec((1, D_MODEL), lambda i: (0, 0)),
            pl.BlockSpec((D_MODEL, n_cols), lambda i: (0, 0)),
        ],
        out_specs=[pl.BlockSpec((tm, w), lambda i: (i, 0)) for w in widths],
        out_shape=[jax.ShapeDtypeStruct((t, w), F32) for w in widths],
        compiler_params=_cparams(("parallel",)),
        name="in_proj",
    )(x, g, w_cat)


def _delta_kernel(qkvz_ref, ab_ref, buf_ref, s0_ref, convw_ref, alog_ref, dtb_ref, ng_ref,
                  ya_ref, sout_ref, xp_sc, qn_sc, kn_sc, vn_sc, gb_sc, s_sc, *, lb, lp, l_valid):
    li = pl.program_id(1)
    n_l = pl.num_programs(1)
    n_chunks = lp // DN_CHUNK

    @pl.when(li == 0)
    def _():
        xp_sc[0:8, :] = buf_ref[0]
        s_sc[...] = s0_ref[0]

    if lp > lb:
        xp_sc[8 + lb:8 + lp, :] = jnp.zeros((lp - lb, CONV_W), F32)
    xp_sc[8:8 + lb, :] = qkvz_ref[0, :, 0:CONV_W]

    for s in range(3 * DN_HEADS):
        c0 = s * HEAD_W
        y = None
        for i in range(DN_CONV):
            term = xp_sc[5 + i:5 + i + lp, c0:c0 + HEAD_W] * convw_ref[i:i + 1, c0:c0 + HEAD_W]
            y = term if y is None else y + term
        y = _silu(y)
        which, h = divmod(s, DN_HEADS)
        if which < 2:
            y = y * lax.rsqrt(jnp.sum(y * y, axis=-1, keepdims=True) + NORM_EPS)
            if which == 0:
                qn_sc[:, h * HEAD_W:(h + 1) * HEAD_W] = y * (DN_DK ** -0.5)
            else:
                kn_sc[:, h * HEAD_W:(h + 1) * HEAD_W] = y
        else:
            vn_sc[:, h * HEAD_W:(h + 1) * HEAD_W] = y

    if lp == lb:
        xp_sc[0:8, :] = xp_sc[lp:lp + 8, :]

    ab = ab_ref[0]
    if lp > lb:
        ab = jnp.concatenate([ab, jnp.zeros((lp - lb, AB_W), F32)], axis=0)
    row = lax.broadcasted_iota(jnp.int32, (lp, AB_W), 0) + li * lb
    lane = lax.broadcasted_iota(jnp.int32, (lp, AB_W), 1)
    g_all = -jnp.exp(alog_ref[...]) * jax.nn.softplus(ab + dtb_ref[...])
    gb = jnp.where(lane < DN_HEADS, g_all, jax.nn.sigmoid(ab))
    gb_sc[...] = jnp.where(row < l_valid, gb, 0.0)

    r = lax.broadcasted_iota(jnp.int32, (STACK, STACK), 0)
    c = lax.broadcasted_iota(jnp.int32, (STACK, STACK), 1)
    ri, ci = r % DN_CHUNK, c % DN_CHUNK
    same = (r // DN_CHUNK) == (c // DN_CHUNK)
    incl = same & (ci <= ri)
    strict = same & (ci < ri)
    lvl16 = strict & ((ri // 16) == (ci // 16))
    lvl32 = strict & ((ri // 32) == (ci // 32)) & ((ri // 16) != (ci // 16))
    lvl64 = strict & ((ri // 32) != (ci // 32))
    eye = (r == c).astype(F32)
    rseg = lax.broadcasted_iota(jnp.int32, (STACK, HEAD_W), 0) % DN_CHUNK

    def stack_heads(ref, t0):
        return jnp.concatenate(
            [ref[pl.ds(t0, DN_CHUNK), h * HEAD_W:(h + 1) * HEAD_W] for h in range(DN_HEADS)], axis=0)

    def lane_bcast(col_block, lane_idx):
        return jnp.broadcast_to(col_block[:, lane_idx:lane_idx + 1], (DN_CHUNK, HEAD_W))

    def chunk(ci_, carry):
        t0 = pl.multiple_of(ci_ * DN_CHUNK, DN_CHUNK)
        q = stack_heads(qn_sc, t0)
        k = stack_heads(kn_sc, t0)
        v = stack_heads(vn_sc, t0)
        gbc = gb_sc[pl.ds(t0, DN_CHUNK), :]
        g = jnp.concatenate([lane_bcast(gbc, h) for h in range(DN_HEADS)], axis=0)
        beta = jnp.concatenate([lane_bcast(gbc, DN_HEADS + h) for h in range(DN_HEADS)], axis=0)

        gcum = g
        for sh in (1, 2, 4, 8, 16, 32):
            gcum = gcum + jnp.where(rseg >= sh, pltpu.roll(gcum, sh, 0), 0.0)
        g_last = jnp.concatenate(
            [jnp.broadcast_to(gcum[h * DN_CHUNK + DN_CHUNK - 1:h * DN_CHUNK + DN_CHUNK, :], (DN_CHUNK, HEAD_W))
             for h in range(DN_HEADS)], axis=0)
        gi = jnp.concatenate([gcum, gcum], axis=1)
        gd = gi - gi.T
        decay = jnp.exp(jnp.where(incl, gd, -jnp.inf))
        e_g = jnp.exp(gcum)

        kb = k * beta
        m = jnp.where(strict, _mm(kb, k, NT) * decay, 0.0)
        qk = _mm(q, k, NT) * decay

        n1 = jnp.where(lvl16, m, 0.0)
        n2 = _mm(n1, n1, passes=3)
        n4 = _mm(n2, n2, passes=3)
        n8 = _mm(n4, n4, passes=3)
        t = eye - n1
        t = t + _mm(t, n2, passes=3)
        t = t + _mm(t, n4, passes=3)
        t = t + _mm(t, n8, passes=3)
        for lvl in (lvl32, lvl64):
            t = t - _mm(_mm(t, jnp.where(lvl, m, 0.0), passes=3), t, passes=3)

        rhs = jnp.concatenate([v * beta, kb * e_g], axis=1)
        sol = _mm(t, rhs, passes=3)
        u0, w = sol[:, :HEAD_W], sol[:, HEAD_W:]
        q_dec = q * e_g
        k_tail = k * jnp.exp(g_last - gcum)
        g_tot = jnp.exp(g_last)

        u_parts = []
        qs_parts = []
        for h in range(DN_HEADS):
            rows = slice(h * DN_CHUNK, (h + 1) * DN_CHUNK)
            s_h = s_sc[h]
            u_parts.append(u0[rows] - _mm(w[rows], s_h, passes=3))
            qs_parts.append(_mm(q_dec[rows], s_h))
        u = jnp.concatenate(u_parts, axis=0)
        o = jnp.concatenate(qs_parts, axis=0) + _mm(qk, u)
        for h in range(DN_HEADS):
            rows = slice(h * DN_CHUNK, (h + 1) * DN_CHUNK)
            upd = _mm(k_tail[rows], u[rows], TN, passes=3)
            s_sc[h] = s_sc[h] * g_tot[h * DN_CHUNK:h * DN_CHUNK + 1, :] + upd
        o = _rms(o, ng_ref[...])
        for h in range(DN_HEADS):
            vn_sc[pl.ds(t0, DN_CHUNK), h * HEAD_W:(h + 1) * HEAD_W] = o[h * DN_CHUNK:(h + 1) * DN_CHUNK]
        return carry

    lax.fori_loop(0, n_chunks, chunk, 0)

    z = qkvz_ref[0, :, CONV_W:QKVZ_W]
    ya_ref[0] = vn_sc[0:lb, :] * _silu(z)

    @pl.when(li == n_l - 1)
    def _():
        sout_ref[0] = s_sc[...]


def _delta_branch(qkvz, ab, conv_buf8, s0, conv_w8, alog, dtb, ng, lb):
    b, l, _ = qkvz.shape
    lp = -(-lb // DN_CHUNK) * DN_CHUNK
    assert l % lb == 0 and (lp == lb or l == lb)
    kern = functools.partial(_delta_kernel, lb=lb, lp=lp, l_valid=l)
    return pl.pallas_call(
        kern,
        grid=(b, l // lb),
        in_specs=[
            pl.BlockSpec((1, lb, QKVZ_W), lambda i, j: (i, j, 0)),
            pl.BlockSpec((1, lb, AB_W), lambda i, j: (i, j, 0)),
            pl.BlockSpec((1, 8, CONV_W), lambda i, j: (i, 0, 0)),
            pl.BlockSpec((1, DN_HEADS, DN_DK, DN_DV), lambda i, j: (i, 0, 0, 0)),
            pl.BlockSpec((8, CONV_W), lambda i, j: (0, 0)),
            pl.BlockSpec((1, AB_W), lambda i, j: (0, 0)),
            pl.BlockSpec((1, AB_W), lambda i, j: (0, 0)),
            pl.BlockSpec((1, HEAD_W), lambda i, j: (0, 0)),
        ],
        out_specs=[
            pl.BlockSpec((1, lb, DN_W), lambda i, j: (i, j, 0)),
            pl.BlockSpec((1, DN_HEADS, DN_DK, DN_DV), lambda i, j: (i, 0, 0, 0)),
        ],
        out_shape=[
            jax.ShapeDtypeStruct((b, l, DN_W), F32),
            jax.ShapeDtypeStruct((b, DN_HEADS, DN_DK, DN_DV), F32),
        ],
        scratch_shapes=[
            pltpu.VMEM((lp + 8, CONV_W), F32),
            pltpu.VMEM((lp, DN_W), F32),
            pltpu.VMEM((lp, DN_W), F32),
            pltpu.VMEM((lp, DN_W), F32),
            pltpu.VMEM((lp, AB_W), F32),
            pltpu.VMEM((DN_HEADS, DN_DK, DN_DV), F32),
        ],
        compiler_params=_cparams(("parallel", "arbitrary")),
        name="delta_rule",
    )(qkvz, ab, conv_buf8, s0, conv_w8, alog, dtb, ng)


def _lambda_from(lvec_ref):
    lv = lvec_ref[...]
    s1 = jnp.sum(lv[0:1] * lv[1:2], axis=-1, keepdims=True)
    s2 = jnp.sum(lv[2:3] * lv[3:4], axis=-1, keepdims=True)
    return jnp.exp(s1) - jnp.exp(s2) + LAM_INIT


def _split_maps(qh):
    qs = qh * (DF_DK ** -0.5)
    lane = lax.broadcasted_iota(jnp.int32, qs.shape, 1)
    return jnp.concatenate([jnp.where(lane < DF_DK, qs, 0.0), jnp.where(lane >= DF_DK, qs, 0.0)],
                           axis=0).astype(BF16)


def _diff_out(acc, l, lam, g, t):
    o = acc[:t] / l[:t] - lam * (acc[t:] / l[t:])
    return _rms(o, g) * (1.0 - LAM_INIT)


def _prompt_attn_kernel(slopes_ref, lvec_ref, g_ref, q_ref, k_ref, v_ref, o_ref, *, tq, tk):
    h = pl.program_id(1)
    qi = pl.program_id(2)
    slope = slopes_ref[h]
    qz = _split_maps(q_ref[0])
    qpos = qi * tq + lax.broadcasted_iota(jnp.int32, (2 * tq, tk), 0) % tq
    col = lax.broadcasted_iota(jnp.int32, (2 * tq, tk), 1)

    def body(j, carry):
        m, l, acc = carry
        k0 = pl.multiple_of(j * tk, tk)
        kb = k_ref[0, pl.ds(k0, tk), :].astype(BF16)
        vb = v_ref[0, pl.ds(k0, tk), :].astype(BF16)
        s = lax.dot_general(qz, kb, NT, preferred_element_type=F32)
        dist = qpos - (k0 + col)
        s = jnp.where(dist >= 0, s - slope * dist.astype(F32), -jnp.inf)
        m_new = jnp.maximum(m, jnp.max(s, axis=-1, keepdims=True))
        alpha = jnp.exp(m - m_new)
        p = jnp.exp(s - m_new)
        l = alpha * l + jnp.sum(p, axis=-1, keepdims=True)
        acc = alpha * acc + jnp.dot(p.astype(BF16), vb, preferred_element_type=F32)
        return m_new, l, acc

    n_kv = (qi * tq + tq + tk - 1) // tk
    init = (jnp.full((2 * tq, 1), -jnp.inf, F32), jnp.zeros((2 * tq, 1), F32), jnp.zeros((2 * tq, DF_DV), F32))
    _, l, acc = lax.fori_loop(0, n_kv, body, init)
    o_ref[0] = _diff_out(acc, l, _lambda_from(lvec_ref), g_ref[...], tq)


def _prompt_attn(slopes, lvec, g, q, k, v, tq, tk):
    b, l, _ = q.shape
    kern = functools.partial(_prompt_attn_kernel, tq=tq, tk=tk)
    return pl.pallas_call(
        kern,
        grid=(b, DF_HEADS, l // tq),
        in_specs=[
            pl.BlockSpec(memory_space=pltpu.SMEM),
            pl.BlockSpec((8, HEAD_W), lambda i, h, j: (0, 0)),
            pl.BlockSpec((1, DF_DV), lambda i, h, j: (0, 0)),
            pl.BlockSpec((1, tq, HEAD_W), lambda i, h, j: (i, j, h)),
            pl.BlockSpec((1, l, HEAD_W), lambda i, h, j: (i, 0, h)),
            pl.BlockSpec((1, l, HEAD_W), lambda i, h, j: (i, 0, h)),
        ],
        out_specs=pl.BlockSpec((1, tq, HEAD_W), lambda i, h, j: (i, j, h)),
        out_shape=jax.ShapeDtypeStruct((b, l, DF_W), F32),
        compiler_params=_cparams(("parallel", "parallel", "arbitrary")),
        name="prompt_attn",
    )(slopes, lvec, g, q, k, v)


def _paged_attn_kernel(pt_ref, slopes_ref, lvec_ref, g_ref, q_ref, kn_ref, vn_ref, *rest, pps, past, t_new):
    k_refs = rest[:pps]
    v_refs = rest[pps:2 * pps]
    o_ref, m_sc, l_sc, acc_sc = rest[2 * pps:]
    j = pl.program_id(1)
    n_j = pl.num_programs(1)
    t2 = 2 * t_new

    @pl.when(j == 0)
    def _():
        m_sc[...] = jnp.full(m_sc.shape, -jnp.inf, F32)
        l_sc[...] = jnp.zeros(l_sc.shape, F32)
        acc_sc[...] = jnp.zeros(acc_sc.shape, F32)

    q = q_ref[0]
    qpos = past + lax.broadcasted_iota(jnp.int32, (t2, PAGE_SIZE), 0) % t_new
    col = lax.broadcasted_iota(jnp.int32, (t2, PAGE_SIZE), 1)

    def update(h, s, v_blocks):
        rows = slice(h * t2, (h + 1) * t2)
        m_old = m_sc[rows, 0:1]
        m_new = jnp.maximum(m_old, jnp.max(s, axis=-1, keepdims=True))
        alpha = jnp.exp(m_old - m_new)
        p = jnp.exp(s - m_new)
        l_new = alpha * l_sc[rows, 0:1] + jnp.sum(p, axis=-1, keepdims=True)
        pv = None
        for i, vb in enumerate(v_blocks):
            d = jnp.dot(p[:, i * PAGE_SIZE:(i + 1) * PAGE_SIZE].astype(BF16), vb, preferred_element_type=F32)
            pv = d if pv is None else pv + d
        acc_sc[rows, :] = alpha * acc_sc[rows, :] + pv
        m_sc[rows, :] = jnp.broadcast_to(m_new, (t2, HEAD_W))
        l_sc[rows, :] = jnp.broadcast_to(l_new, (t2, HEAD_W))

    for h in range(DF_HEADS):
        hs = slice(h * HEAD_W, (h + 1) * HEAD_W)
        qz = _split_maps(q[:, hs])
        slope = slopes_ref[h]
        s_blocks, v_blocks = [], []
        for i in range(pps):
            kb = k_refs[i][0, 0, :, hs].astype(BF16)
            s = lax.dot_general(qz, kb, NT, preferred_element_type=F32)
            kpos = (j * pps + i) * PAGE_SIZE + col
            s_blocks.append(s - slope * (qpos - kpos).astype(F32))
            v_blocks.append(v_refs[i][0, 0, :, hs].astype(BF16))
        update(h, jnp.concatenate(s_blocks, axis=1), v_blocks)

    @pl.when(j == n_j - 1)
    def _():
        lam = _lambda_from(lvec_ref)
        pad = jnp.zeros((PAGE_SIZE - t_new, HEAD_W), F32)
        for h in range(DF_HEADS):
            hs = slice(h * HEAD_W, (h + 1) * HEAD_W)
            qz = _split_maps(q[:, hs])
            kb = jnp.concatenate([kn_ref[0, :, hs], pad], axis=0).astype(BF16)
            vb = jnp.concatenate([vn_ref[0, :, hs], pad], axis=0).astype(BF16)
            s = lax.dot_general(qz, kb, NT, preferred_element_type=F32)
            dist = qpos - (past + col)
            s = jnp.where((dist >= 0) & (col < t_new), s - slopes_ref[h] * dist.astype(F32), -jnp.inf)
            update(h, s, [vb])
            rows = slice(h * t2, (h + 1) * t2)
            o_ref[0, :, hs] = _diff_out(acc_sc[rows, :], l_sc[rows, 0:1], lam, g_ref[...], t_new)


def _paged_attn(page_table, slopes, lvec, g, q, k_new, v_new, cache_k, cache_v, pps):
    b, t_new, _ = q.shape
    n_pages = page_table.shape[1]
    assert n_pages % pps == 0
    kern = functools.partial(_paged_attn_kernel, pps=pps, past=n_pages * PAGE_SIZE, t_new=t_new)

    def page_spec(i):
        return pl.BlockSpec((1, 1, PAGE_SIZE, DF_W),
                            lambda bi, j, pt, i=i: (0, pt[bi * n_pages + j * pps + i], 0, 0))

    tok_spec = pl.BlockSpec((1, t_new, DF_W), lambda bi, j, pt: (bi, 0, 0))
    grid_spec = pltpu.PrefetchScalarGridSpec(
        num_scalar_prefetch=1,
        grid=(b, n_pages // pps),
        in_specs=[
            pl.BlockSpec(memory_space=pltpu.SMEM),
            pl.BlockSpec((8, HEAD_W), lambda bi, j, pt: (0, 0)),
            pl.BlockSpec((1, DF_DV), lambda bi, j, pt: (0, 0)),
            tok_spec, tok_spec, tok_spec,
        ] + [page_spec(i) for i in range(pps)] * 2,
        out_specs=tok_spec,
        scratch_shapes=[
            pltpu.VMEM((DF_HEADS * 2 * t_new, HEAD_W), F32),
            pltpu.VMEM((DF_HEADS * 2 * t_new, HEAD_W), F32),
            pltpu.VMEM((DF_HEADS * 2 * t_new, DF_DV), F32),
        ],
    )
    return pl.pallas_call(
        kern,
        grid_spec=grid_spec,
        out_shape=jax.ShapeDtypeStruct((b, t_new, DF_W), F32),
        compiler_params=_cparams(("parallel", "arbitrary")),
        name="paged_attn",
    )(page_table.reshape(-1), slopes, lvec, g, q, k_new, v_new, *([cache_k] * pps), *([cache_v] * pps))


def _mix_kernel(ya_ref, ob_ref, gates_ref, x_ref, wa_ref, wb_ref, wo_ref, gf_ref, wr_ref, br_ref,
                h_ref, hn_ref, route_ref):
    ga = jax.nn.sigmoid(gates_ref[:, :D_MODEL])
    gb = jax.nn.sigmoid(gates_ref[:, D_MODEL:])
    a = jnp.dot(ya_ref[...].astype(BF16), wa_ref[...], preferred_element_type=F32)
    b = jnp.dot(ob_ref[...].astype(BF16), wb_ref[...], preferred_element_type=F32)
    mixed = ga * a + gb * b
    h = x_ref[...] + jnp.dot(mixed.astype(BF16), wo_ref[...], preferred_element_type=F32)
    h_ref[...] = h
    hn = _rms(h, gf_ref[...])
    hn_ref[...] = hn.astype(BF16)

    logits = _mm(hn, wr_ref[...], passes=3) + br_ref[...]
    lane = lax.broadcasted_iota(jnp.int32, logits.shape, 1)
    neg = -jnp.inf
    big = 1 << 20
    is_g = lane < N_GROUPS
    gl = jnp.where(is_g, logits, neg)
    g_max = jnp.max(gl, axis=-1, keepdims=True)
    g_idx = jnp.min(jnp.where(is_g & (logits == g_max), lane, big), axis=-1, keepdims=True)
    g_w = 1.0 / jnp.sum(jnp.exp(gl - g_max), axis=-1, keepdims=True)
    e_lo = N_GROUPS + g_idx * EXPERTS_PER_GROUP
    in_grp = (lane >= e_lo) & (lane < e_lo + EXPERTS_PER_GROUP)
    el = jnp.where(in_grp, logits, neg)
    e1 = jnp.max(el, axis=-1, keepdims=True)
    i1 = jnp.min(jnp.where(in_grp & (logits == e1), lane, big), axis=-1, keepdims=True)
    el2 = jnp.where(lane == i1, neg, el)
    e2 = jnp.max(el2, axis=-1, keepdims=True)
    i2 = jnp.min(jnp.where(in_grp & (lane != i1) & (logits == e2), lane, big), axis=-1, keepdims=True)
    d = jnp.exp(e2 - e1)
    w1 = g_w / (1.0 + d)
    w2 = g_w * d / (1.0 + d)
    route = jnp.where(lane == i1, w1, 0.0) + jnp.where(lane == i2, w2, 0.0)
    route_ref[...] = route


def _mix(ya, ob, gates, x, wa, wb, wo, gf, wr, br, tm):
    t = x.shape[0]
    row = lambda w: pl.BlockSpec((tm, w), lambda i: (i, 0))
    full = lambda a: pl.BlockSpec(a.shape, lambda i: (0,) * a.ndim)
    return pl.pallas_call(
        _mix_kernel,
        grid=(t // tm,),
        in_specs=[row(DN_W), row(DF_W), row(GATE_W), row(D_MODEL),
                  full(wa), full(wb), full(wo), full(gf), full(wr), full(br)],
        out_specs=[row(D_MODEL), row(D_MODEL), row(128)],
        out_shape=[jax.ShapeDtypeStruct((t, D_MODEL), F32), jax.ShapeDtypeStruct((t, D_MODEL), BF16),
                   jax.ShapeDtypeStruct((t, 128), F32)],
        compiler_params=_cparams(("parallel",)),
        name="branch_mix",
    )(ya, ob, gates, x, wa, wb, wo, gf, wr, br)


def _moe_kernel(hn_ref, route_ref, h_ref, wgu_ref, wd_ref, gfin_ref, y_ref, acc_sc):
    e = pl.program_id(1)

    @pl.when(e == 0)
    def _():
        acc_sc[...] = jnp.zeros(acc_sc.shape, F32)

    gu = jnp.dot(hn_ref[...], wgu_ref[0], preferred_element_type=F32)
    act = _silu(gu[:, :D_EXPERT]) * gu[:, D_EXPERT:]
    lane = lax.broadcasted_iota(jnp.int32, route_ref.shape, 1)
    gate = jnp.sum(jnp.where(lane == e + N_GROUPS, route_ref[...], 0.0), axis=-1, keepdims=True)
    acc_sc[...] += jnp.dot((act * gate).astype(BF16), wd_ref[0], preferred_element_type=F32)

    @pl.when(e == pl.num_programs(1) - 1)
    def _():
        y_ref[...] = _rms(h_ref[...] + acc_sc[...], gfin_ref[...])


def _moe(hn, route, h, wgu, wd, gfin, tm):
    t = h.shape[0]
    return pl.pallas_call(
        _moe_kernel,
        grid=(t // tm, N_EXPERTS),
        in_specs=[
            pl.BlockSpec((tm, D_MODEL), lambda i, e: (i, 0)),
            pl.BlockSpec((tm, 128), lambda i, e: (i, 0)),
            pl.BlockSpec((tm, D_MODEL), lambda i, e: (i, 0)),
            pl.BlockSpec((1, D_MODEL, 2 * D_EXPERT), lambda i, e: (e, 0, 0)),
            pl.BlockSpec((1, D_EXPERT, D_MODEL), lambda i, e: (e, 0, 0)),
            pl.BlockSpec((1, D_MODEL), lambda i, e: (0, 0)),
        ],
        out_specs=pl.BlockSpec((tm, D_MODEL), lambda i, e: (i, 0)),
        out_shape=jax.ShapeDtypeStruct((t, D_MODEL), F32),
        scratch_shapes=[pltpu.VMEM((tm, D_MODEL), F32)],
        compiler_params=_cparams(("parallel", "arbitrary")),
        name="moe_experts",
    )(hn, route, h, wgu, wd, gfin)


def _row(v, width):
    v = v.reshape(1, -1).astype(F32)
    return jnp.pad(v, ((0, 0), (0, width - v.shape[1])))


def _forward(x, conv_buf, s0, attend, w, tiles):
    b, l, _ = x.shape
    t = b * l
    tiles = {name: min(size, t) if name != "delta" else size for name, size in tiles.items()}
    gates, qkvz, qdf, kdf, vdf, ab = _in_proj(x.reshape(t, D_MODEL), w["norm_mix_g"], w["w_cat"], tiles["proj"])
    qkvz3 = qkvz.reshape(b, l, QKVZ_W)
    buf8 = jnp.pad(conv_buf, ((0, 0), (8 - (DN_CONV - 1), 0), (0, 0)))
    ya, s_new = _delta_branch(qkvz3, ab.reshape(b, l, AB_W), buf8, s0, w["conv_w8"], w["alog"], w["dtb"],
                              w["dn_norm_g"], tiles["delta"])
    new_buf = qkvz3[:, l - (DN_CONV - 1):, :CONV_W]
    q3, k3, v3 = (a.reshape(b, l, DF_W) for a in (qdf, kdf, vdf))
    ob = attend(q3, k3, v3)
    h, hn, route = _mix(ya.reshape(t, DN_W), ob.reshape(t, DF_W), gates, x.reshape(t, D_MODEL),
                        w["wa"], w["wb"], w["wo"], w["norm_ffn_g"], w["wr"], w["br"], tiles["mix"])
    y = _moe(hn, route, h, w["wgu"], w["wd"], w["final_g"], tiles["moe"])
    return (y.reshape(b, l, D_MODEL), s_new[None], new_buf[None],
            k3.reshape(1, b, l, DF_HEADS, 2 * DF_DK), v3.reshape(1, b, l, DF_HEADS, DF_DV))


def kernel(x_prompt, x_sample, cache_k, cache_v, page_table, state_delta, state_conv, norm_mix_g, w_in, conv_w,
           dn_a_log, dn_dt_bias, dn_norm_g, df_lambda_q1, df_lambda_k1, df_lambda_q2, df_lambda_k2, df_norm_g,
           w_branch_a, w_branch_b, w_out, norm_ffn_g, w_group, b_group, w_router, b_router, w_gate_up, w_down,
           final_norm_g):
    w_in0 = w_in[0]
    o_z = 4 * DN_W
    o_ab = o_z + 2 * DN_HEADS
    o_df = o_ab + 3 * DF_W
    w_cat = jnp.concatenate(
        [w_in0[:, o_df:], w_in0[:, :o_z], w_in0[:, o_ab:o_df],
         jnp.pad(w_in0[:, o_z:o_ab], ((0, 0), (0, AB_W - 2 * DN_HEADS)))], axis=1).astype(BF16)
    w = {
        "norm_mix_g": _row(norm_mix_g[0], D_MODEL),
        "w_cat": w_cat,
        "conv_w8": jnp.pad(conv_w[0], ((0, 8 - DN_CONV), (0, 0))),
        "alog": _row(dn_a_log[0], AB_W),
        "dtb": _row(dn_dt_bias[0], AB_W),
        "dn_norm_g": _row(dn_norm_g[0], HEAD_W),
        "wa": w_branch_a[0].astype(BF16),
        "wb": w_branch_b[0].astype(BF16),
        "wo": w_out[0].astype(BF16),
        "norm_ffn_g": _row(norm_ffn_g[0], D_MODEL),
        "wr": jnp.pad(jnp.concatenate([w_group[0], w_router[0]], axis=1),
                      ((0, 0), (0, 128 - N_GROUPS - N_EXPERTS))),
        "br": _row(jnp.concatenate([b_group[0], b_router[0]]), 128),
        "wgu": w_gate_up[0].astype(BF16),
        "wd": w_down[0].astype(BF16),
        "final_g": _row(final_norm_g, D_MODEL),
    }
    slopes = 2.0 ** (-8.0 * jnp.arange(1, DF_HEADS + 1, dtype=F32) / DF_HEADS)
    lvec = jnp.pad(jnp.stack([df_lambda_q1[0], df_lambda_k1[0], df_lambda_q2[0], df_lambda_k2[0]]),
                   ((0, 4), (0, HEAD_W - DF_DK)))
    g_df = _row(df_norm_g[0], DF_DV)

    bp, lp_, _ = x_prompt.shape
    bs, ls, _ = x_sample.shape

    def prompt_attend(q, k, v):
        return _prompt_attn(slopes, lvec, g_df, q, k, v, tq=256, tk=512)

    ck = cache_k.reshape(cache_k.shape[0], cache_k.shape[1], PAGE_SIZE, DF_W)
    cv = cache_v.reshape(cache_v.shape[0], cache_v.shape[1], PAGE_SIZE, DF_W)

    def paged_attend(q, k, v):
        return _paged_attn(page_table, slopes, lvec, g_df, q, k, v, ck, cv, pps=8)

    tiles_p = {"proj": 256, "delta": 512, "mix": 256, "moe": 512}
    tiles_s = {"proj": 256, "delta": ls, "mix": 256, "moe": 512}
    zero_buf = jnp.zeros((bp, DN_CONV - 1, CONV_W), F32)
    zero_s = jnp.zeros((bp, DN_HEADS, DN_DK, DN_DV), F32)
    yp, sd_p, sc_p, k_p, v_p = _forward(x_prompt, zero_buf, zero_s, prompt_attend, w, tiles_p)
    ys, sd_s, sc_s, k_s, v_s = _forward(x_sample, state_conv[0], state_delta[0], paged_attend, w, tiles_s)
    return (yp, ys, sd_p, sc_p, k_p, v_p, sd_s, sc_s, k_s, v_s)
```

```python
import functools
import math

import jax
import jax.numpy as jnp
from jax import lax
from jax.experimental import pallas as pl
from jax.experimental.pallas import tpu as pltpu

F32 = jnp.float32
BF16 = jnp.bfloat16

D_MODEL = 1024
PAGE_SIZE = 128
DN_HEADS = 4
DN_DK = 128
DN_DV = 128
DN_CONV = 4
DN_CHUNK = 64
DF_HEADS = 4
DF_DK = 64
DF_DV = 128
N_GROUPS = 4
EXPERTS_PER_GROUP = 4
N_EXPERTS = 16
D_EXPERT = 512
NORM_EPS = 1e-6
LAM_INIT = 0.8 - 0.6 * math.exp(-0.3 * 0)
LOG2E = 1.0 / math.log(2.0)

HEAD_W = 128
DN_W = DN_HEADS * HEAD_W
DF_W = DF_HEADS * HEAD_W
QKVZ_W = 4 * DN_W
CONV_W = 3 * DN_W
GATE_W = 2 * D_MODEL
AB_W = 128
STACK = DN_HEADS * DN_CHUNK
DN_CHUNKS_IN_FLIGHT = 4
VMEM_LIMIT = 52 * 1024 * 1024


def _cparams(sem):
    return pltpu.CompilerParams(dimension_semantics=sem, vmem_limit_bytes=VMEM_LIMIT)


def _split_bf16(a):
    hi = a.astype(BF16)
    lo = (a - hi.astype(F32)).astype(BF16)
    return hi, lo


def _mm(a, b, dims=(((1,), (0,)), ((), ())), passes=1):
    if passes == 1:
        return lax.dot_general(a.astype(BF16), b.astype(BF16), dims, preferred_element_type=F32)
    a_hi, a_lo = _split_bf16(a)
    b_hi, b_lo = _split_bf16(b)
    out = lax.dot_general(a_hi, b_hi, dims, preferred_element_type=F32)
    out = out + lax.dot_general(a_hi, b_lo, dims, preferred_element_type=F32)
    out = out + lax.dot_general(a_lo, b_hi, dims, preferred_element_type=F32)
    return out


NT = (((1,), (1,)), ((), ()))
TN = (((0,), (0,)), ((), ()))


def _rms(x, g):
    return x * lax.rsqrt(jnp.mean(x * x, axis=-1, keepdims=True) + NORM_EPS) * g


def _silu(x):
    return x * jax.nn.sigmoid(x)


def _in_proj_kernel(x_ref, g_ref, w_ref, gates_ref, qkvz_ref, qdf_ref, kdf_ref, vdf_ref, ab_ref):
    xn = _rms(x_ref[...], g_ref[...]).astype(BF16)

    def proj(off, out_ref):
        width = out_ref.shape[1]
        for c in range(0, width, 512):
            w = min(512, width - c)
            out_ref[:, c:c + w] = jnp.dot(xn, w_ref[:, off + c:off + c + w], preferred_element_type=F32)
        return off + width

    off = 0
    for ref in (gates_ref, qkvz_ref, qdf_ref, kdf_ref, vdf_ref, ab_ref):
        off = proj(off, ref)


def _in_proj(x, g, w_cat, tm):
    t = x.shape[0]
    widths = (GATE_W, QKVZ_W, DF_W, DF_W, DF_W, AB_W)
    n_cols = sum(widths)
    return pl.pallas_call(
        _in_proj_kernel,
        grid=(t // tm,),
        in_specs=[
            pl.BlockSpec((tm, D_MODEL), lambda i: (i, 0)),
            pl.BlockSpec((1, D_MODEL), lambda i: (0, 0)),
            pl.BlockSpec((D_MODEL, n_cols), lambda i: (0, 0)),
        ],
        out_specs=[pl.BlockSpec((tm, w), lambda i: (i, 0)) for w in widths],
        out_shape=[jax.ShapeDtypeStruct((t, w), F32) for w in widths],
        compiler_params=_cparams(("parallel",)),
        name="in_proj",
    )(x, g, w_cat)


def _delta_kernel(qkvz_ref, ab_ref, buf_ref, s0_ref, convw_ref, alog_ref, dtb_ref, ng_ref,
                  ya_ref, sout_ref, xp_sc, qn_sc, kn_sc, vn_sc, o_sc, gb_sc, s_sc, *, n_seq, lb, lp_seq):
    li = pl.program_id(1)
    n_l = pl.num_programs(1)
    lp = n_seq * lp_seq
    n_chunks = lp // DN_CHUNK
    chunks_per_seq = lp_seq // DN_CHUNK

    @pl.when(li == 0)
    def _():
        s_sc[...] = s0_ref[...]
        if n_seq == 1:
            xp_sc[0:8, :] = buf_ref[0]

    if lp_seq > lb:
        xp_sc[...] = jnp.zeros(xp_sc.shape, F32)
    for i in range(n_seq):
        if n_seq > 1:
            xp_sc[i * lp_seq:i * lp_seq + 8, :] = buf_ref[i]
        xp_sc[8 + i * lp_seq:8 + i * lp_seq + lb, :] = qkvz_ref[i, :, 0:CONV_W]

    for s in range(3 * DN_HEADS):
        c0 = s * HEAD_W
        y = None
        for i in range(DN_CONV):
            term = xp_sc[5 + i:5 + i + lp, c0:c0 + HEAD_W] * convw_ref[i:i + 1, c0:c0 + HEAD_W]
            y = term if y is None else y + term
        y = _silu(y)
        which, h = divmod(s, DN_HEADS)
        if which < 2:
            y = y * lax.rsqrt(jnp.sum(y * y, axis=-1, keepdims=True) + NORM_EPS)
            if which == 0:
                qn_sc[:, h * HEAD_W:(h + 1) * HEAD_W] = y * (DN_DK ** -0.5)
            else:
                kn_sc[:, h * HEAD_W:(h + 1) * HEAD_W] = y
        else:
            vn_sc[:, h * HEAD_W:(h + 1) * HEAD_W] = y

    if n_seq == 1 and lp_seq == lb:
        xp_sc[0:8, :] = xp_sc[lp:lp + 8, :]

    if lp_seq > lb:
        gb_sc[...] = jnp.zeros(gb_sc.shape, F32)
    lane = lax.broadcasted_iota(jnp.int32, (lb, AB_W), 1)
    for i in range(n_seq):
        ab = ab_ref[i]
        g_all = -jnp.exp(alog_ref[...]) * jax.nn.softplus(ab + dtb_ref[...])
        gb_sc[i * lp_seq:i * lp_seq + lb, :] = jnp.where(lane < DN_HEADS, g_all, jax.nn.sigmoid(ab))

    r = lax.broadcasted_iota(jnp.int32, (STACK, STACK), 0)
    c = lax.broadcasted_iota(jnp.int32, (STACK, STACK), 1)
    ri, ci = r % DN_CHUNK, c % DN_CHUNK
    same = (r // DN_CHUNK) == (c // DN_CHUNK)
    incl = same & (ci <= ri)
    strict = same & (ci < ri)
    lvl16 = strict & ((ri // 16) == (ci // 16))
    lvl32 = strict & ((ri // 32) == (ci // 32)) & ((ri // 16) != (ci // 16))
    lvl64 = strict & ((ri // 32) != (ci // 32))
    eye = (r == c).astype(F32)
    rseg = lax.broadcasted_iota(jnp.int32, (STACK, HEAD_W), 0) % DN_CHUNK

    def stack_heads(ref, t0):
        return jnp.concatenate(
            [ref[pl.ds(t0, DN_CHUNK), h * HEAD_W:(h + 1) * HEAD_W] for h in range(DN_HEADS)], axis=0)

    def lane_bcast(col_block, lane_idx):
        return jnp.broadcast_to(col_block[:, lane_idx:lane_idx + 1], (DN_CHUNK, HEAD_W))

    n_par = math.gcd(n_chunks, DN_CHUNKS_IN_FLIGHT)

    def each(f, *lists):
        return [f(*xs) for xs in zip(*lists)]

    def trip(ti, carry):
        t0s = [pl.multiple_of((ti * n_par + i) * DN_CHUNK, DN_CHUNK) for i in range(n_par)]
        q = each(lambda t0: stack_heads(qn_sc, t0), t0s)
        k = each(lambda t0: stack_heads(kn_sc, t0), t0s)
        v = each(lambda t0: stack_heads(vn_sc, t0), t0s)
        gbc = each(lambda t0: gb_sc[pl.ds(t0, DN_CHUNK), :], t0s)
        g = each(lambda x: jnp.concatenate([lane_bcast(x, h) for h in range(DN_HEADS)], axis=0), gbc)
        beta = each(lambda x: jnp.concatenate([lane_bcast(x, DN_HEADS + h) for h in range(DN_HEADS)], axis=0), gbc)

        gcum = g
        for sh in (1, 2, 4, 8, 16, 32):
            gcum = each(lambda x: x + jnp.where(rseg >= sh, pltpu.roll(x, sh, 0), 0.0), gcum)
        g_last = each(lambda x: jnp.concatenate(
            [jnp.broadcast_to(x[h * DN_CHUNK + DN_CHUNK - 1:h * DN_CHUNK + DN_CHUNK, :], (DN_CHUNK, HEAD_W))
             for h in range(DN_HEADS)], axis=0), gcum)

        def decay_of(x):
            gi = jnp.concatenate([x, x], axis=1)
            return jnp.exp(jnp.where(incl, gi - gi.T, -jnp.inf))

        decay = each(decay_of, gcum)
        e_g = each(jnp.exp, gcum)
        kb = each(lambda a, b: a * b, k, beta)
        m = each(lambda a, b, d: jnp.where(strict, _mm(a, b, NT) * d, 0.0), kb, k, decay)
        qk = each(lambda a, b, d: _mm(a, b, NT) * d, q, k, decay)

        mm3 = lambda a, b: _mm(a, b, passes=3)
        n1 = each(lambda x: jnp.where(lvl16, x, 0.0), m)
        n2 = each(mm3, n1, n1)
        n4 = each(mm3, n2, n2)
        n8 = each(mm3, n4, n4)
        t = each(lambda x: eye - x, n1)
        for n in (n2, n4, n8):
            t = each(lambda a, b: a + mm3(a, b), t, n)
        for lvl in (lvl32, lvl64):
            tl = each(lambda a, x: mm3(a, jnp.where(lvl, x, 0.0)), t, m)
            t = each(lambda a, b: a - mm3(b, a), t, tl)

        rhs = each(lambda a, b, c, d: jnp.concatenate([a * b, c * d], axis=1), v, beta, kb, e_g)
        sol = each(mm3, t, rhs)
        q_dec = each(lambda a, b: a * b, q, e_g)
        k_tail = each(lambda a, gl, gc: a * jnp.exp(gl - gc), k, g_last, gcum)
        g_tot = each(jnp.exp, g_last)

        for i in range(n_par):
            seq = 0 if n_seq == 1 else (ti * n_par + i) // chunks_per_seq
            u0, w = sol[i][:, :HEAD_W], sol[i][:, HEAD_W:]
            u_parts = []
            qs_parts = []
            for h in range(DN_HEADS):
                rows = slice(h * DN_CHUNK, (h + 1) * DN_CHUNK)
                s_h = s_sc[seq, h]
                u_parts.append(u0[rows] - _mm(w[rows], s_h, passes=3))
                qs_parts.append(_mm(q_dec[i][rows], s_h))
            u = jnp.concatenate(u_parts, axis=0)
            o = jnp.concatenate(qs_parts, axis=0) + _mm(qk[i], u)
            for h in range(DN_HEADS):
                rows = slice(h * DN_CHUNK, (h + 1) * DN_CHUNK)
                upd = _mm(k_tail[i][rows], u[rows], TN, passes=3)
                s_sc[seq, h] = s_sc[seq, h] * g_tot[i][h * DN_CHUNK:h * DN_CHUNK + 1, :] + upd
            o = _rms(o, ng_ref[...])
            for h in range(DN_HEADS):
                o_sc[pl.ds(t0s[i], DN_CHUNK), h * HEAD_W:(h + 1) * HEAD_W] = o[h * DN_CHUNK:(h + 1) * DN_CHUNK]
        return carry

    lax.fori_loop(0, n_chunks // n_par, trip, 0)

    for i in range(n_seq):
        ya_ref[i] = o_sc[i * lp_seq:i * lp_seq + lb, :] * _silu(qkvz_ref[i, :, CONV_W:QKVZ_W])

    @pl.when(li == n_l - 1)
    def _():
        sout_ref[...] = s_sc[...]


def _delta_branch(qkvz, ab, conv_buf8, s0, conv_w8, alog, dtb, ng, lb, n_seq):
    b, l, _ = qkvz.shape
    lp_seq = -(-lb // DN_CHUNK) * DN_CHUNK
    lp = n_seq * lp_seq
    assert l % lb == 0 and b % n_seq == 0 and ((lp_seq == lb and n_seq == 1) or l == lb)
    kern = functools.partial(_delta_kernel, n_seq=n_seq, lb=lb, lp_seq=lp_seq)
    return pl.pallas_call(
        kern,
        grid=(b // n_seq, l // lb),
        in_specs=[
            pl.BlockSpec((n_seq, lb, QKVZ_W), lambda i, j: (i, j, 0)),
            pl.BlockSpec((n_seq, lb, AB_W), lambda i, j: (i, j, 0)),
            pl.BlockSpec((n_seq, 8, CONV_W), lambda i, j: (i, 0, 0)),
            pl.BlockSpec((n_seq, DN_HEADS, DN_DK, DN_DV), lambda i, j: (i, 0, 0, 0)),
            pl.BlockSpec((8, CONV_W), lambda i, j: (0, 0)),
            pl.BlockSpec((1, AB_W), lambda i, j: (0, 0)),
            pl.BlockSpec((1, AB_W), lambda i, j: (0, 0)),
            pl.BlockSpec((1, HEAD_W), lambda i, j: (0, 0)),
        ],
        out_specs=[
            pl.BlockSpec((n_seq, lb, DN_W), lambda i, j: (i, j, 0)),
            pl.BlockSpec((n_seq, DN_HEADS, DN_DK, DN_DV), lambda i, j: (i, 0, 0, 0)),
        ],
        out_shape=[
            jax.ShapeDtypeStruct((b, l, DN_W), F32),
            jax.ShapeDtypeStruct((b, DN_HEADS, DN_DK, DN_DV), F32),
        ],
        scratch_shapes=[
            pltpu.VMEM((lp + 8, CONV_W), F32),
            pltpu.VMEM((lp, DN_W), F32),
            pltpu.VMEM((lp, DN_W), F32),
            pltpu.VMEM((lp, DN_W), F32),
            pltpu.VMEM((lp, DN_W), F32),
            pltpu.VMEM((lp, AB_W), F32),
            pltpu.VMEM((n_seq, DN_HEADS, DN_DK, DN_DV), F32),
        ],
        compiler_params=_cparams(("parallel", "arbitrary")),
        name="delta_rule",
    )(qkvz, ab, conv_buf8, s0, conv_w8, alog, dtb, ng)


def _lambda_from(lvec_ref):
    lv = lvec_ref[...]
    s1 = jnp.sum(lv[0:1] * lv[1:2], axis=-1, keepdims=True)
    s2 = jnp.sum(lv[2:3] * lv[3:4], axis=-1, keepdims=True)
    return jnp.exp(s1) - jnp.exp(s2) + LAM_INIT


def _split_maps(qh):
    qs = qh * (DF_DK ** -0.5)
    lane = lax.broadcasted_iota(jnp.int32, qs.shape, 1)
    return jnp.concatenate([jnp.where(lane < DF_DK, qs, 0.0), jnp.where(lane >= DF_DK, qs, 0.0)],
                           axis=0).astype(BF16)


def _diff_out(acc, l, lam, g, t):
    o = acc[:t] / l[:t] - lam * (acc[t:] / l[t:])
    return _rms(o, g) * (1.0 - LAM_INIT)


def _prompt_attn_kernel(slopes_ref, lvec_ref, g_ref, q_ref, k_ref, v_ref, o_ref, *, t):
    h = pl.program_id(1)
    qi = pl.program_id(2)
    slope2 = slopes_ref[h] * LOG2E
    qz = _split_maps(q_ref[0] * LOG2E)
    row = lax.broadcasted_iota(jnp.int32, (2 * t, t), 0) % t
    col = lax.broadcasted_iota(jnp.int32, (2 * t, t), 1)
    bias = slope2 * (col - row).astype(F32)

    def step(j, carry, tile_bias):
        m, l, acc = carry
        k0 = pl.multiple_of(j * t, t)
        kb = k_ref[0, pl.ds(k0, t), :].astype(BF16)
        vb = v_ref[0, pl.ds(k0, t), :].astype(BF16)
        s = lax.dot_general(qz, kb, NT, preferred_element_type=F32) + tile_bias
        shift = slope2 * ((j - qi) * t).astype(F32)
        m_new = jnp.maximum(m, jnp.max(s, axis=-1, keepdims=True) + shift)
        alpha = jnp.exp2(m - m_new)
        p = jnp.exp2(s - (m_new - shift))
        l = alpha * l + jnp.sum(p, axis=-1, keepdims=True)
        acc = alpha * acc + jnp.dot(p.astype(BF16), vb, preferred_element_type=F32)
        return m_new, l, acc

    init = (jnp.full((2 * t, 1), -jnp.inf, F32), jnp.zeros((2 * t, 1), F32), jnp.zeros((2 * t, DF_DV), F32))
    carry = lax.fori_loop(0, qi, lambda j, c: step(j, c, bias), init)
    _, l, acc = step(qi, carry, jnp.where(col <= row, bias, -jnp.inf))
    o_ref[0] = _diff_out(acc, l, _lambda_from(lvec_ref), g_ref[...], t)


def _prompt_attn(slopes, lvec, g, q, k, v, tq):
    b, l, _ = q.shape
    kern = functools.partial(_prompt_attn_kernel, t=tq)
    return pl.pallas_call(
        kern,
        grid=(b, DF_HEADS, l // tq),
        in_specs=[
            pl.BlockSpec(memory_space=pltpu.SMEM),
            pl.BlockSpec((8, HEAD_W), lambda i, h, j: (0, 0)),
            pl.BlockSpec((1, DF_DV), lambda i, h, j: (0, 0)),
            pl.BlockSpec((1, tq, HEAD_W), lambda i, h, j: (i, j, h)),
            pl.BlockSpec((1, l, HEAD_W), lambda i, h, j: (i, 0, h)),
            pl.BlockSpec((1, l, HEAD_W), lambda i, h, j: (i, 0, h)),
        ],
        out_specs=pl.BlockSpec((1, tq, HEAD_W), lambda i, h, j: (i, j, h)),
        out_shape=jax.ShapeDtypeStruct((b, l, DF_W), F32),
        compiler_params=_cparams(("parallel", "parallel", "arbitrary")),
        name="prompt_attn",
    )(slopes, lvec, g, q, k, v)


def _head_slopes(rows, t2):
    h = lax.broadcasted_iota(jnp.int32, (rows, 1), 0) // t2
    slope = jnp.full((rows, 1), 2.0 ** (-8.0 / DF_HEADS), F32)
    for i in range(1, DF_HEADS):
        slope = jnp.where(h == i, 2.0 ** (-8.0 * (i + 1) / DF_HEADS), slope)
    return slope


def _paged_attn_kernel(pt_ref, lvec_ref, g_ref, q_ref, kn_ref, vn_ref, *rest, pps, past, t_new):
    k_refs = rest[:pps]
    v_refs = rest[pps:2 * pps]
    o_ref, qz_sc, m_sc, l_sc, acc_sc = rest[2 * pps:]
    j = pl.program_id(1)
    n_j = pl.num_programs(1)
    t2 = 2 * t_new
    rows = DF_HEADS * t2

    def head_rows(h):
        return slice(h * t2, (h + 1) * t2)

    def head_lanes(h):
        return slice(h * HEAD_W, (h + 1) * HEAD_W)

    @pl.when(j == 0)
    def _():
        q = q_ref[0]
        qz_sc[...] = jnp.zeros(qz_sc.shape, BF16)
        for h in range(DF_HEADS):
            qz_sc[head_rows(h), head_lanes(h)] = _split_maps(q[:, head_lanes(h)])
        m_sc[...] = jnp.full(m_sc.shape, -jnp.inf, F32)
        l_sc[...] = jnp.zeros(l_sc.shape, F32)
        acc_sc[...] = jnp.zeros(acc_sc.shape, F32)

    slope = _head_slopes(rows, t2)
    t_row = lax.broadcasted_iota(jnp.int32, (rows, 1), 0) % t_new

    def scores(keys):
        s_t = lax.dot_general(keys, qz_sc[...], NT, preferred_element_type=F32)
        return s_t.T[:rows]

    def update(s, v_of):
        n = s.shape[1] // PAGE_SIZE
        m_old = m_sc[...]
        m_new = jnp.maximum(m_old, jnp.max(s, axis=-1, keepdims=True))
        alpha = jnp.exp(m_old - m_new)
        p = jnp.exp(s - m_new[:, 0:1])
        l_sc[...] = alpha * l_sc[...] + jnp.sum(p, axis=-1, keepdims=True)
        pb = p.astype(BF16)
        pv = []
        for h in range(DF_HEADS):
            acc_h = None
            for i in range(n):
                d = jnp.dot(pb[head_rows(h), i * PAGE_SIZE:(i + 1) * PAGE_SIZE], v_of(h, i),
                            preferred_element_type=F32)
                acc_h = d if acc_h is None else acc_h + d
            pv.append(acc_h)
        acc_sc[...] = alpha * acc_sc[...] + jnp.concatenate(pv, axis=0)
        m_sc[...] = m_new

    def head_of_page(ref, h):
        return ref[pl.ds(h, PAGE_SIZE, stride=DF_HEADS), :].astype(BF16)

    def page_keys(i):
        return jnp.concatenate([head_of_page(k_refs[i], h) for h in range(DF_HEADS)], axis=1)

    halves = [range(0, pps // 2), range(pps // 2, pps)] if pps > 1 else [range(pps)]
    s = jnp.concatenate(
        [scores(jnp.concatenate([page_keys(i) for i in half], axis=0)) for half in halves], axis=1)
    col = lax.broadcasted_iota(jnp.int32, s.shape, 1)
    dist = (past - j * (pps * PAGE_SIZE)) + t_row - col
    update(s - slope * dist.astype(F32), lambda h, i: head_of_page(v_refs[i], h))

    @pl.when(j == n_j - 1)
    def _():
        pad = jnp.zeros((PAGE_SIZE - t_new, HEAD_W), F32)
        vn = [jnp.concatenate([vn_ref[0, :, head_lanes(h)], pad], axis=0).astype(BF16) for h in range(DF_HEADS)]
        s_new = scores(jnp.concatenate([kn_ref[0], jnp.zeros((PAGE_SIZE - t_new, DF_W), F32)], axis=0).astype(BF16))
        col_n = lax.broadcasted_iota(jnp.int32, s_new.shape, 1)
        dist_n = t_row - col_n
        s_new = jnp.where((dist_n >= 0) & (col_n < t_new), s_new - slope * dist_n.astype(F32), -jnp.inf)
        update(s_new, lambda h, i: vn[h])
        lam = _lambda_from(lvec_ref)
        for h in range(DF_HEADS):
            o_ref[0, :, head_lanes(h)] = _diff_out(acc_sc[head_rows(h), :], l_sc[head_rows(h), 0:1], lam,
                                                   g_ref[...], t_new)


def _paged_attn(page_table, lvec, g, q, k_new, v_new, cache_k, cache_v, pps):
    b, t_new, _ = q.shape
    n_pages = page_table.shape[1]
    assert n_pages % pps == 0
    kern = functools.partial(_paged_attn_kernel, pps=pps, past=n_pages * PAGE_SIZE, t_new=t_new)

    def page_spec(i):
        return pl.BlockSpec((None, None, PAGE_SIZE * DF_HEADS, HEAD_W),
                            lambda bi, j, pt, i=i: (0, pt[bi * n_pages + j * pps + i], 0, 0))

    tok_spec = pl.BlockSpec((1, t_new, DF_W), lambda bi, j, pt: (bi, 0, 0))
    rows = DF_HEADS * 2 * t_new
    grid_spec = pltpu.PrefetchScalarGridSpec(
        num_scalar_prefetch=1,
        grid=(b, n_pages // pps),
        in_specs=[
            pl.BlockSpec((8, HEAD_W), lambda bi, j, pt: (0, 0)),
            pl.BlockSpec((1, DF_DV), lambda bi, j, pt: (0, 0)),
            tok_spec, tok_spec, tok_spec,
        ] + [page_spec(i) for i in range(pps)] * 2,
        out_specs=tok_spec,
        scratch_shapes=[
            pltpu.VMEM((HEAD_W, DF_W), BF16),
            pltpu.VMEM((rows, HEAD_W), F32),
            pltpu.VMEM((rows, HEAD_W), F32),
            pltpu.VMEM((rows, DF_DV), F32),
        ],
    )
    return pl.pallas_call(
        kern,
        grid_spec=grid_spec,
        out_shape=jax.ShapeDtypeStruct((b, t_new, DF_W), F32),
        compiler_params=_cparams(("parallel", "arbitrary")),
        name="paged_attn",
    )(page_table.reshape(-1), lvec, g, q, k_new, v_new, *([cache_k] * pps), *([cache_v] * pps))


def _mix_kernel(ya_ref, ob_ref, gates_ref, x_ref, wa_ref, wb_ref, wo_ref, gf_ref, wr_ref, br_ref,
                h_ref, hn_ref, route_ref):
    ga = jax.nn.sigmoid(gates_ref[:, :D_MODEL])
    gb = jax.nn.sigmoid(gates_ref[:, D_MODEL:])
    a = jnp.dot(ya_ref[...].astype(BF16), wa_ref[...], preferred_element_type=F32)
    b = jnp.dot(ob_ref[...].astype(BF16), wb_ref[...], preferred_element_type=F32)
    mixed = ga * a + gb * b
    h = x_ref[...] + jnp.dot(mixed.astype(BF16), wo_ref[...], preferred_element_type=F32)
    h_ref[...] = h
    hn = _rms(h, gf_ref[...])
    hn_ref[...] = hn.astype(BF16)

    logits = _mm(hn, wr_ref[...], passes=3) + br_ref[...]
    lane = lax.broadcasted_iota(jnp.int32, logits.shape, 1)
    neg = -jnp.inf
    big = 1 << 20
    is_g = lane < N_GROUPS
    gl = jnp.where(is_g, logits, neg)
    g_max = jnp.max(gl, axis=-1, keepdims=True)
    g_idx = jnp.min(jnp.where(is_g & (logits == g_max), lane, big), axis=-1, keepdims=True)
    g_w = 1.0 / jnp.sum(jnp.exp(gl - g_max), axis=-1, keepdims=True)
    e_lo = N_GROUPS + g_idx * EXPERTS_PER_GROUP
    in_grp = (lane >= e_lo) & (lane < e_lo + EXPERTS_PER_GROUP)
    el = jnp.where(in_grp, logits, neg)
    e1 = jnp.max(el, axis=-1, keepdims=True)
    i1 = jnp.min(jnp.where(in_grp & (logits == e1), lane, big), axis=-1, keepdims=True)
    el2 = jnp.where(lane == i1, neg, el)
    e2 = jnp.max(el2, axis=-1, keepdims=True)
    i2 = jnp.min(jnp.where(in_grp & (lane != i1) & (logits == e2), lane, big), axis=-1, keepdims=True)
    d = jnp.exp(e2 - e1)
    w1 = g_w / (1.0 + d)
    w2 = g_w * d / (1.0 + d)
    route = jnp.where(lane == i1, w1, 0.0) + jnp.where(lane == i2, w2, 0.0)
    route_ref[...] = route


def _mix(ya, ob, gates, x, wa, wb, wo, gf, wr, br, tm):
    t = x.shape[0]
    row = lambda w: pl.BlockSpec((tm, w), lambda i: (i, 0))
    full = lambda a: pl.BlockSpec(a.shape, lambda i: (0,) * a.ndim)
    return pl.pallas_call(
        _mix_kernel,
        grid=(t // tm,),
        in_specs=[row(DN_W), row(DF_W), row(GATE_W), row(D_MODEL),
                  full(wa), full(wb), full(wo), full(gf), full(wr), full(br)],
        out_specs=[row(D_MODEL), row(D_MODEL), row(128)],
        out_shape=[jax.ShapeDtypeStruct((t, D_MODEL), F32), jax.ShapeDtypeStruct((t, D_MODEL), BF16),
                   jax.ShapeDtypeStruct((t, 128), F32)],
        compiler_params=_cparams(("parallel",)),
        name="branch_mix",
    )(ya, ob, gates, x, wa, wb, wo, gf, wr, br)


def _moe_kernel(hn_ref, route_ref, h_ref, wgu_ref, wd_ref, gfin_ref, y_ref, acc_sc):
    e = pl.program_id(1)

    @pl.when(e == 0)
    def _():
        acc_sc[...] = jnp.zeros(acc_sc.shape, F32)

    gu = jnp.dot(hn_ref[...], wgu_ref[0], preferred_element_type=F32)
    act = _silu(gu[:, :D_EXPERT]) * gu[:, D_EXPERT:]
    lane = lax.broadcasted_iota(jnp.int32, route_ref.shape, 1)
    gate = jnp.sum(jnp.where(lane == e + N_GROUPS, route_ref[...], 0.0), axis=-1, keepdims=True)
    acc_sc[...] += jnp.dot((act * gate).astype(BF16), wd_ref[0], preferred_element_type=F32)

    @pl.when(e == pl.num_programs(1) - 1)
    def _():
        y_ref[...] = _rms(h_ref[...] + acc_sc[...], gfin_ref[...])


def _moe(hn, route, h, wgu, wd, gfin, tm):
    t = h.shape[0]
    return pl.pallas_call(
        _moe_kernel,
        grid=(t // tm, N_EXPERTS),
        in_specs=[
            pl.BlockSpec((tm, D_MODEL), lambda i, e: (i, 0)),
            pl.BlockSpec((tm, 128), lambda i, e: (i, 0)),
            pl.BlockSpec((tm, D_MODEL), lambda i, e: (i, 0)),
            pl.BlockSpec((1, D_MODEL, 2 * D_EXPERT), lambda i, e: (e, 0, 0)),
            pl.BlockSpec((1, D_EXPERT, D_MODEL), lambda i, e: (e, 0, 0)),
            pl.BlockSpec((1, D_MODEL), lambda i, e: (0, 0)),
        ],
        out_specs=pl.BlockSpec((tm, D_MODEL), lambda i, e: (i, 0)),
        out_shape=jax.ShapeDtypeStruct((t, D_MODEL), F32),
        scratch_shapes=[pltpu.VMEM((tm, D_MODEL), F32)],
        compiler_params=_cparams(("parallel", "arbitrary")),
        name="moe_experts",
    )(hn, route, h, wgu, wd, gfin)


def _row(v, width):
    v = v.reshape(1, -1).astype(F32)
    return jnp.pad(v, ((0, 0), (0, width - v.shape[1])))


def _forward(x, conv_buf, s0, attend, w, tiles):
    b, l, _ = x.shape
    t = b * l
    tiles = {name: size if name.startswith("delta") else min(size, t) for name, size in tiles.items()}
    gates, qkvz, qdf, kdf, vdf, ab = _in_proj(x.reshape(t, D_MODEL), w["norm_mix_g"], w["w_cat"], tiles["proj"])
    qkvz3 = qkvz.reshape(b, l, QKVZ_W)
    buf8 = jnp.pad(conv_buf, ((0, 0), (8 - (DN_CONV - 1), 0), (0, 0)))
    ya, s_new = _delta_branch(qkvz3, ab.reshape(b, l, AB_W), buf8, s0, w["conv_w8"], w["alog"], w["dtb"],
                              w["dn_norm_g"], tiles["delta"], tiles["delta_seqs"])
    new_buf = qkvz3[:, l - (DN_CONV - 1):, :CONV_W]
    q3, k3, v3 = (a.reshape(b, l, DF_W) for a in (qdf, kdf, vdf))
    ob = attend(q3, k3, v3)
    h, hn, route = _mix(ya.reshape(t, DN_W), ob.reshape(t, DF_W), gates, x.reshape(t, D_MODEL),
                        w["wa"], w["wb"], w["wo"], w["norm_ffn_g"], w["wr"], w["br"], tiles["mix"])
    y = _moe(hn, route, h, w["wgu"], w["wd"], w["final_g"], tiles["moe"])
    return (y.reshape(b, l, D_MODEL), s_new[None], new_buf[None],
            k3.reshape(1, b, l, DF_HEADS, 2 * DF_DK), v3.reshape(1, b, l, DF_HEADS, DF_DV))


def kernel(x_prompt, x_sample, cache_k, cache_v, page_table, state_delta, state_conv, norm_mix_g, w_in, conv_w,
           dn_a_log, dn_dt_bias, dn_norm_g, df_lambda_q1, df_lambda_k1, df_lambda_q2, df_lambda_k2, df_norm_g,
           w_branch_a, w_branch_b, w_out, norm_ffn_g, w_group, b_group, w_router, b_router, w_gate_up, w_down,
           final_norm_g):
    w_in0 = w_in[0]
    o_z = 4 * DN_W
    o_ab = o_z + 2 * DN_HEADS
    o_df = o_ab + 3 * DF_W
    w_cat = jnp.concatenate(
        [w_in0[:, o_df:], w_in0[:, :o_z], w_in0[:, o_ab:o_df],
         jnp.pad(w_in0[:, o_z:o_ab], ((0, 0), (0, AB_W - 2 * DN_HEADS)))], axis=1).astype(BF16)
    w = {
        "norm_mix_g": _row(norm_mix_g[0], D_MODEL),
        "w_cat": w_cat,
        "conv_w8": jnp.pad(conv_w[0], ((0, 8 - DN_CONV), (0, 0))),
        "alog": _row(dn_a_log[0], AB_W),
        "dtb": _row(dn_dt_bias[0], AB_W),
        "dn_norm_g": _row(dn_norm_g[0], HEAD_W),
        "wa": w_branch_a[0].astype(BF16),
        "wb": w_branch_b[0].astype(BF16),
        "wo": w_out[0].astype(BF16),
        "norm_ffn_g": _row(norm_ffn_g[0], D_MODEL),
        "wr": jnp.pad(jnp.concatenate([w_group[0], w_router[0]], axis=1),
                      ((0, 0), (0, 128 - N_GROUPS - N_EXPERTS))),
        "br": _row(jnp.concatenate([b_group[0], b_router[0]]), 128),
        "wgu": w_gate_up[0].astype(BF16),
        "wd": w_down[0].astype(BF16),
        "final_g": _row(final_norm_g, D_MODEL),
    }
    slopes = 2.0 ** (-8.0 * jnp.arange(1, DF_HEADS + 1, dtype=F32) / DF_HEADS)
    lvec = jnp.pad(jnp.stack([df_lambda_q1[0], df_lambda_k1[0], df_lambda_q2[0], df_lambda_k2[0]]),
                   ((0, 4), (0, HEAD_W - DF_DK)))
    g_df = _row(df_norm_g[0], DF_DV)

    bp, lp_, _ = x_prompt.shape
    bs, ls, _ = x_sample.shape

    def prompt_attend(q, k, v):
        return _prompt_attn(slopes, lvec, g_df, q, k, v, tq=min(512, q.shape[1]))

    ck = cache_k.reshape(cache_k.shape[0], cache_k.shape[1], PAGE_SIZE * DF_HEADS, HEAD_W)
    cv = cache_v.reshape(cache_v.shape[0], cache_v.shape[1], PAGE_SIZE * DF_HEADS, HEAD_W)

    def paged_attend(q, k, v):
        return _paged_attn(page_table, lvec, g_df, q, k, v, ck, cv, pps=min(16, page_table.shape[1]))

    tiles_p = {"proj": 256, "delta": min(512, lp_), "delta_seqs": 1, "mix": 256, "moe": 512}
    tiles_s = {"proj": 256, "delta": ls, "delta_seqs": math.gcd(bs, DN_CHUNKS_IN_FLIGHT), "mix": 256, "moe": 512}
    zero_buf = jnp.zeros((bp, DN_CONV - 1, CONV_W), F32)
    zero_s = jnp.zeros((bp, DN_HEADS, DN_DK, DN_DV), F32)
    yp, sd_p, sc_p, k_p, v_p = _forward(x_prompt, zero_buf, zero_s, prompt_attend, w, tiles_p)
    ys, sd_s, sc_s, k_s, v_s = _forward(x_sample, state_conv[0], state_delta[0], paged_attend, w, tiles_s)
    return (yp, ys, sd_p, sc_p, k_p, v_p, sd_s, sc_s, k_s, v_s)
```
